```python
import functools
import jax, jax.numpy as jnp
from jax import lax
import numpy as np

D_MODEL = 1024
BATCH = 2
SEQ = 8192
DEPTH = 4
DEC_BATCH = 128
DEC_SEQ = 1
PAST_LEN = 2048
PAGE_SIZE = 128

N_HEADS = 8
HEAD_DIM = D_MODEL // N_HEADS
D_ATTN = N_HEADS * HEAD_DIM
Q_BLOCK = 128
FORGET_BIAS_INIT = 3.0
D_RNN = D_MODEL
N_RNN_BLOCKS = 8
RNN_BLOCK = D_RNN // N_RNN_BLOCKS
CONV_W = 4
RG_C = 8.0
N_EXPERTS = 16
N_GROUPS = 4
EXPERTS_PER_GROUP = N_EXPERTS // N_GROUPS
TOP_K = 2
D_EXPERT = D_MODEL // 2
ALPHA = (2 * DEPTH) ** 0.25
BETA = (8 * DEPTH) ** -0.25
LN_EPS = 1e-5
COL_SIZES = (D_RNN, D_RNN, D_ATTN, D_ATTN, D_ATTN, N_HEADS, D_MODEL, D_MODEL)
D_IN = sum(COL_SIZES)

kernel_name = "hawk_fox_parallel_grouped_moe_step"


def layer_norm(x, g, b):
    xf = x.astype(jnp.float32)
    mu = jnp.mean(xf, axis=-1, keepdims=True)
    var = jnp.mean(jnp.square(xf - mu), axis=-1, keepdims=True)
    return ((xf - mu) * lax.rsqrt(var + LN_EPS)).astype(x.dtype) * g + b


def split_cols(z):
    offs = [sum(COL_SIZES[:i + 1]) for i in range(len(COL_SIZES) - 1)]
    return jnp.split(z, offs, axis=-1)


def causal_conv(u, prefix, w, b):
    s = u.shape[1]
    up = jnp.concatenate([prefix.astype(u.dtype), u], axis=1)
    out = b + sum(up[:, k:k + s] * w[k] for k in range(CONV_W))
    return out, up[:, -(CONV_W - 1):]


def rg_lru(u, h0, w_r, b_r, w_i, b_i, lam):
    bsz, s, _ = u.shape
    ub = u.reshape(bsz, s, N_RNN_BLOCKS, RNN_BLOCK)
    r = jax.nn.sigmoid(jnp.einsum('bsnc,ncd->bsnd', ub, w_r).reshape(bsz, s, D_RNN) + b_r)
    i = jax.nn.sigmoid(jnp.einsum('bsnc,ncd->bsnd', ub, w_i).reshape(bsz, s, D_RNN) + b_i)
    log_a = -RG_C * r.astype(jnp.float32) * jax.nn.softplus(-lam.astype(jnp.float32))
    a = jnp.exp(log_a)
    inp = (i * u).astype(jnp.float32) * jnp.sqrt(-jnp.expm1(2.0 * log_a))
    inp = inp.at[:, 0].add(a[:, 0] * h0.astype(jnp.float32))

    def combine(left, right):
        al, bl = left
        ar, br = right
        return ar * al, ar * bl + br

    _, h = lax.associative_scan(combine, (a, inp), axis=1)
    return h.astype(u.dtype), h[:, -1].astype(u.dtype)


def fox_prompt(q, k, v, logf):
    bsz, s, h, dh = q.shape
    nblk = s // Q_BLOCK
    ct = jnp.cumsum(logf, axis=1).transpose(0, 2, 1)
    qb = q.reshape(bsz, nblk, Q_BLOCK, h, dh).transpose(1, 0, 2, 3, 4)
    cb = ct.reshape(bsz, h, nblk, Q_BLOCK).transpose(2, 0, 1, 3)
    kpos = jnp.arange(s)
    scale = dh ** -0.5

    def one_block(args):
        qi, ci, blk = args
        sc = jnp.einsum('bqhd,bkhd->bhqk', qi, k, preferred_element_type=jnp.float32) * scale
        sc = sc + ci[..., None] - ct[:, :, None, :]
        qpos = blk * Q_BLOCK + jnp.arange(Q_BLOCK)
        sc = jnp.where(kpos[None, :] <= qpos[:, None], sc, -jnp.inf)
        p = jax.nn.softmax(sc, axis=-1)
        return jnp.einsum('bhqk,bkhd->bqhd', p.astype(v.dtype), v)

    o = lax.map(one_block, (qb, cb, jnp.arange(nblk)))
    return o.transpose(1, 0, 2, 3, 4).reshape(bsz, s, h, dh)


def fox_sample(k_past, v_past, logf_past, q, k_new, v_new, logf_new):
    p_len = k_past.shape[1]
    t_len = q.shape[1]
    k = jnp.concatenate([k_past.astype(k_new.dtype), k_new], axis=1)
    v = jnp.concatenate([v_past.astype(v_new.dtype), v_new], axis=1)
    lf = jnp.concatenate([logf_past.astype(jnp.float32), logf_new], axis=1)
    ct = jnp.cumsum(lf, axis=1).transpose(0, 2, 1)
    sc = jnp.einsum('bqhd,bkhd->bhqk', q, k, preferred_element_type=jnp.float32) * HEAD_DIM ** -0.5
    sc = sc + ct[:, :, p_len:, None] - ct[:, :, None, :]
    mask = jnp.arange(p_len + t_len)[None, :] <= (p_len + jnp.arange(t_len))[:, None]
    sc = jnp.where(mask, sc, -jnp.inf)
    p = jax.nn.softmax(sc, axis=-1)
    return jnp.einsum('bhqk,bkhd->bqhd', p.astype(v.dtype), v)


def token_mixer(x, conv_prefix, h0, attend, w_in, conv_w, conv_b, w_rg_r, b_rg_r, w_rg_i, b_rg_i,
                rg_lambda, b_forget, w_branch_a, w_branch_b, w_out):
    bsz, s, _ = x.shape
    u, yg, q, k, v, f, ga, gb = split_cols(x @ w_in)
    uc, new_prefix = causal_conv(u, conv_prefix, conv_w, conv_b)
    h, h_last = rg_lru(uc, h0, w_rg_r, b_rg_r, w_rg_i, b_rg_i, rg_lambda)
    branch_a = (h * jax.nn.gelu(yg)) @ w_branch_a
    logf = jax.nn.log_sigmoid(f.astype(jnp.float32) + b_forget.astype(jnp.float32))
    q = q.reshape(bsz, s, N_HEADS, HEAD_DIM)
    k = k.reshape(bsz, s, N_HEADS, HEAD_DIM)
    v = v.reshape(bsz, s, N_HEADS, HEAD_DIM)
    o = attend(q, k, v, logf)
    branch_b = o.reshape(bsz, s, D_ATTN) @ w_branch_b
    merged = jax.nn.sigmoid(ga) * branch_a + jax.nn.sigmoid(gb) * branch_b
    return merged @ w_out, (k, v, logf.astype(x.dtype), h_last, new_prefix)


def grouped_moe(x, w_router, b_router, w_e1, w_e3, w_e2):
    bsz, s, d = x.shape
    xt = x.reshape(-1, d)
    logits = (xt @ w_router).astype(jnp.float32) + b_router.astype(jnp.float32)
    probs = jax.nn.softmax(logits, axis=-1)
    gscore = jnp.sum(lax.top_k(probs.reshape(-1, N_GROUPS, EXPERTS_PER_GROUP), TOP_K)[0], axis=-1)
    gsel = jnp.argmax(gscore, axis=-1)
    in_group = (jnp.arange(N_EXPERTS) // EXPERTS_PER_GROUP)[None, :] == gsel[:, None]
    top_logit, top_idx = lax.top_k(jnp.where(in_group, logits, -jnp.inf), TOP_K)
    top_w = jax.nn.softmax(top_logit, axis=-1)
    gate = jnp.sum(jax.nn.one_hot(top_idx, N_EXPERTS, dtype=jnp.float32) * top_w[..., None], axis=1)
    gate = gate.astype(x.dtype)
    y = jnp.zeros_like(xt)
    for e in range(N_EXPERTS):
        hdn = jax.nn.silu(xt @ w_e1[e]) * (xt @ w_e3[e])
        y = y + gate[:, e:e + 1] * (hdn @ w_e2[e])
    return y.reshape(bsz, s, d)


def setup_inputs(seed: int = 0) -> dict:
    key = jax.random.key(seed)
    ks = iter(jax.random.split(key, 40))

    def nrm(shape, scale):
        return jax.random.normal(next(ks), shape, jnp.float32) * scale

    n_pages = PAST_LEN // PAGE_SIZE
    n_used = DEC_BATCH * n_pages
    n_pool = n_used + max(1, n_used // 4)

    x_prompt = nrm((BATCH, SEQ, D_MODEL), 1.0)
    x_sample = nrm((DEC_BATCH, DEC_SEQ, D_MODEL), 1.0)
    cache_k = nrm((DEPTH, n_pool, PAGE_SIZE, N_HEADS, HEAD_DIM), 1.0)
    cache_v = nrm((DEPTH, n_pool, PAGE_SIZE, N_HEADS, HEAD_DIM), BETA)
    cache_logf = jax.nn.log_sigmoid(nrm((DEPTH, n_pool, PAGE_SIZE, N_HEADS), 1.0) + FORGET_BIAS_INIT)
    state_h = nrm((DEPTH, DEC_BATCH, D_RNN), 0.5)
    state_conv = nrm((DEPTH, DEC_BATCH, CONV_W - 1, D_RNN), 1.0)
    page_table = jax.random.permutation(next(ks), n_pool)[:n_used].reshape(DEC_BATCH, n_pages).astype(jnp.int32)

    col = jnp.arange(D_IN)
    v_lo = 2 * D_RNN + 2 * D_ATTN
    col_scale = jnp.where((col >= v_lo) & (col < v_lo + D_ATTN), BETA, 1.0).astype(jnp.float32)
    w_in = nrm((DEPTH, D_MODEL, D_IN), D_MODEL ** -0.5) * col_scale
    conv_w = nrm((DEPTH, CONV_W, D_RNN), CONV_W ** -0.5)
    conv_b = nrm((DEPTH, D_RNN), 0.01)
    w_rg_r = nrm((DEPTH, N_RNN_BLOCKS, RNN_BLOCK, RNN_BLOCK), RNN_BLOCK ** -0.5)
    b_rg_r = nrm((DEPTH, D_RNN), 0.01)
    w_rg_i = nrm((DEPTH, N_RNN_BLOCKS, RNN_BLOCK, RNN_BLOCK), RNN_BLOCK ** -0.5)
    b_rg_i = nrm((DEPTH, D_RNN), 0.01)
    a_c = jax.random.uniform(next(ks), (DEPTH, D_RNN), jnp.float32, 0.9, 0.999)
    a0 = a_c ** (1.0 / RG_C)
    rg_lambda = jnp.log(a0) - jnp.log1p(-a0)
    b_forget = FORGET_BIAS_INIT + nrm((DEPTH, N_HEADS), 0.1)
    w_branch_a = nrm((DEPTH, D_RNN, D_MODEL), D_RNN ** -0.5)
    w_branch_b = nrm((DEPTH, D_ATTN, D_MODEL), D_ATTN ** -0.5)
    w_out = nrm((DEPTH, D_MODEL, D_MODEL), D_MODEL ** -0.5 * BETA)
    ln_mix_g = 1.0 + nrm((DEPTH, D_MODEL), 0.02)
    ln_mix_b = nrm((DEPTH, D_MODEL), 0.02)
    w_router = nrm((D_MODEL, N_EXPERTS), D_MODEL ** -0.5)
    b_router = nrm((N_EXPERTS,), 0.01)
    w_e1 = nrm((DEPTH, N_EXPERTS, D_MODEL, D_EXPERT), D_MODEL ** -0.5)
    w_e3 = nrm((DEPTH, N_EXPERTS, D_MODEL, D_EXPERT), D_MODEL ** -0.5)
    w_e2 = nrm((DEPTH, N_EXPERTS, D_EXPERT, D_MODEL), D_EXPERT ** -0.5 * BETA)
    ln_ffn_g = 1.0 + nrm((DEPTH, D_MODEL), 0.02)
    ln_ffn_b = nrm((DEPTH, D_MODEL), 0.02)
    return {"x_prompt": x_prompt, "x_sample": x_sample, "cache_k": cache_k, "cache_v": cache_v,
            "cache_logf": cache_logf, "state_h": state_h, "state_conv": state_conv,
            "page_table": page_table, "w_in": w_in, "conv_w": conv_w, "conv_b": conv_b,
            "w_rg_r": w_rg_r, "b_rg_r": b_rg_r, "w_rg_i": w_rg_i, "b_rg_i": b_rg_i,
            "rg_lambda": rg_lambda, "b_forget": b_forget, "w_branch_a": w_branch_a,
            "w_branch_b": w_branch_b, "w_out": w_out, "ln_mix_g": ln_mix_g, "ln_mix_b": ln_mix_b,
            "w_router": w_router, "b_router": b_router, "w_e1": w_e1, "w_e3": w_e3, "w_e2": w_e2,
            "ln_ffn_g": ln_ffn_g, "ln_ffn_b": ln_ffn_b}


def reference(x_prompt, x_sample, cache_k, cache_v, cache_logf, state_h, state_conv, page_table,
              w_in, conv_w, conv_b, w_rg_r, b_rg_r, w_rg_i, b_rg_i, rg_lambda, b_forget,
              w_branch_a, w_branch_b, w_out, ln_mix_g, ln_mix_b, w_router, b_router,
              w_e1, w_e3, w_e2, ln_ffn_g, ln_ffn_b):
    xp, xs = x_prompt, x_sample
    nb_p = xp.shape[0]
    nb_s = xs.shape[0]
    past_len = page_table.shape[1] * cache_k.shape[2]
    kp, vp, lfp, hp, cp = [], [], [], [], []
    ksm, vsm, lfs, hs, cs = [], [], [], [], []
    for l in range(DEPTH):
        mix_w = (w_in[l], conv_w[l], conv_b[l], w_rg_r[l], b_rg_r[l], w_rg_i[l], b_rg_i[l],
                 rg_lambda[l], b_forget[l], w_branch_a[l], w_branch_b[l], w_out[l])
        zero_prefix = jnp.zeros((nb_p, CONV_W - 1, D_RNN), xp.dtype)
        zero_h = jnp.zeros((nb_p, D_RNN), xp.dtype)
        mp, st_p = token_mixer(xp, zero_prefix, zero_h, fox_prompt, *mix_w)
        k_past = cache_k[l][page_table].reshape(nb_s, past_len, N_HEADS, HEAD_DIM)
        v_past = cache_v[l][page_table].reshape(nb_s, past_len, N_HEADS, HEAD_DIM)
        lf_past = cache_logf[l][page_table].reshape(nb_s, past_len, N_HEADS)
        attend_s = functools.partial(fox_sample, k_past, v_past, lf_past)
        ms, st_s = token_mixer(xs, state_conv[l], state_h[l], attend_s, *mix_w)
        xp = layer_norm(ALPHA * xp + mp, ln_mix_g[l], ln_mix_b[l])
        xs = layer_norm(ALPHA * xs + ms, ln_mix_g[l], ln_mix_b[l])
        xp = layer_norm(ALPHA * xp + grouped_moe(xp, w_router, b_router, w_e1[l], w_e3[l], w_e2[l]),
                        ln_ffn_g[l], ln_ffn_b[l])
        xs = layer_norm(ALPHA * xs + grouped_moe(xs, w_router, b_router, w_e1[l], w_e3[l], w_e2[l]),
                        ln_ffn_g[l], ln_ffn_b[l])
        kp.append(st_p[0]); vp.append(st_p[1]); lfp.append(st_p[2]); hp.append(st_p[3]); cp.append(st_p[4])
        ksm.append(st_s[0]); vsm.append(st_s[1]); lfs.append(st_s[2]); hs.append(st_s[3]); cs.append(st_s[4])
    return (xp, xs, jnp.stack(kp), jnp.stack(vp), jnp.stack(lfp), jnp.stack(hp), jnp.stack(cp),
            jnp.stack(ksm), jnp.stack(vsm), jnp.stack(lfs), jnp.stack(hs), jnp.stack(cs))
```

```python
import functools

import jax
import jax.numpy as jnp
from jax import lax
from jax.experimental import pallas as pl
from jax.experimental.pallas import tpu as pltpu

F32 = jnp.float32
BF16 = jnp.bfloat16

LANES = 128
SUBLANES = 8
VMEM_LIMIT_BYTES = 56 * 1024 * 1024

N_HEADS = 8
HEAD_DIM = 128
N_RNN_BLOCKS = 8
RNN_BLOCK = 128
CONV_W = 4
RG_C = 8.0
N_EXPERTS = 16
N_GROUPS = 4
EXPERTS_PER_GROUP = N_EXPERTS // N_GROUPS
DEPTH = 4
ALPHA = (2 * DEPTH) ** 0.25
LN_EPS = 1e-5
ATTN_SCALE = HEAD_DIM ** -0.5

COL_U, COL_YG, COL_Q, COL_K, COL_V, COL_GA, COL_GB = range(7)

NT_DIMS = (((1,), (1,)), ((), ()))


def _params(semantics):
    return pltpu.CompilerParams(dimension_semantics=semantics, vmem_limit_bytes=VMEM_LIMIT_BYTES)


def _mm(a, b, precise):
    if precise:
        return jnp.dot(a.astype(F32), b, precision=lax.Precision.HIGHEST, preferred_element_type=F32)
    return jnp.dot(a.astype(BF16), b, preferred_element_type=F32)


def _split3(x):
    hi = x.astype(BF16)
    r1 = x - hi.astype(F32)
    mid = r1.astype(BF16)
    lo = (r1 - mid.astype(F32)).astype(BF16)
    return hi, mid, lo


def _log_sigmoid(x):
    return jnp.minimum(x, 0.0) - jnp.log1p(jnp.exp(-jnp.abs(x)))


def _softplus(x):
    return jnp.maximum(x, 0.0) + jnp.log1p(jnp.exp(-jnp.abs(x)))


def _layer_norm(x, g, b):
    mu = jnp.mean(x, axis=-1, keepdims=True)
    xc = x - mu
    var = jnp.mean(xc * xc, axis=-1, keepdims=True)
    return xc * lax.rsqrt(var + LN_EPS) * g + b


def _inproj_kernel(x_ref, w_ref, wf_ref, bf_ref, z_ref, lf_ref, c_ref, xb_s, carry_s, *, tiles_per_seq,
                   precise):
    i = pl.program_id(0)
    j = pl.program_id(1)

    @pl.when(j == 0)
    def _():
        xb = x_ref[...].astype(xb_s.dtype)
        xb_s[...] = xb
        f = _mm(xb, wf_ref[...], precise) + bf_ref[...]
        lf = _log_sigmoid(f)
        lf_ref[...] = lf

        @pl.when(i % tiles_per_seq == 0)
        def _():
            carry_s[...] = jnp.zeros_like(carry_s)

        tm = lf.shape[0]
        row = lax.broadcasted_iota(jnp.int32, (tm, tm), 0)
        col = lax.broadcasted_iota(jnp.int32, (tm, tm), 1)
        tri = jnp.where(col <= row, 1.0, 0.0).astype(BF16)
        hi, mid, lo = _split3(lf)
        c = (jnp.dot(tri, hi, preferred_element_type=F32)
             + jnp.dot(tri, mid, preferred_element_type=F32)
             + jnp.dot(tri, lo, preferred_element_type=F32)) + carry_s[...]
        c_ref[...] = c
        carry_s[...] = c[tm - 1:tm, :]

    z_ref[...] = _mm(xb_s[...], w_ref[...], precise)


def _inproj(x, w_main, w_f, b_f, *, layer, seq_len, tm, tn):
    n, d = x.shape
    d_out = w_main.shape[-1]
    precise = w_main.dtype == F32
    if seq_len is None:
        tm = min(tm, n)
        tiles_per_seq = 1
    else:
        tm = min(tm, seq_len)
        tiles_per_seq = seq_len // tm
    kern = functools.partial(_inproj_kernel, tiles_per_seq=tiles_per_seq, precise=precise)
    return pl.pallas_call(
        kern,
        grid=(n // tm, d_out // tn),
        in_specs=[
            pl.BlockSpec((tm, d), lambda i, j: (i, 0)),
            pl.BlockSpec((None, d, tn), lambda i, j: (layer, 0, j)),
            pl.BlockSpec((None, d, LANES), lambda i, j: (layer, 0, 0)),
            pl.BlockSpec((None, 1, LANES), lambda i, j: (layer, 0, 0)),
        ],
        out_specs=[
            pl.BlockSpec((tm, tn), lambda i, j: (i, j)),
            pl.BlockSpec((tm, LANES), lambda i, j: (i, 0)),
            pl.BlockSpec((tm, LANES), lambda i, j: (i, 0)),
        ],
        out_shape=[
            jax.ShapeDtypeStruct((n, d_out), F32),
            jax.ShapeDtypeStruct((n, LANES), F32),
            jax.ShapeDtypeStruct((n, LANES), F32),
        ],
        scratch_shapes=[pltpu.VMEM((tm, d), w_main.dtype), pltpu.VMEM((1, LANES), F32)],
        compiler_params=_params(("arbitrary", "arbitrary")),
    )(x, w_main, w_f, b_f)


def _rg_gates(uc, wr_ref, wi_ref, br, bi, sp_neg_lam):
    precise = wr_ref.dtype == F32
    ucb = uc.astype(wr_ref.dtype)
    rs, gs = [], []
    for n in range(N_RNN_BLOCKS):
        blk = ucb[:, n * RNN_BLOCK:(n + 1) * RNN_BLOCK]
        rs.append(_mm(blk, wr_ref[n], precise))
        gs.append(_mm(blk, wi_ref[n], precise))
    r = jax.nn.sigmoid(jnp.concatenate(rs, axis=1) + br)
    ig = jax.nn.sigmoid(jnp.concatenate(gs, axis=1) + bi)
    log_a = (-RG_C) * r * sp_neg_lam
    a = jnp.exp(log_a)
    inp = (ig * uc) * jnp.sqrt(1.0 - a * a)
    return a, inp


def _rglru_seq_kernel(u_ref, yg_ref, cw_ref, cb_ref, wr_ref, wi_ref, br_ref, bi_ref, lam_ref,
                      hg_ref, hlast_ref, ubuf, abuf, bbuf, hcar, cin_s, *, tc, rb):
    t = pl.program_id(1)
    d = u_ref.shape[-1]
    seg = tc // SUBLANES
    nc = d // LANES

    @pl.when(t == 0)
    def _():
        ubuf[0:SUBLANES, :] = jnp.zeros((SUBLANES, d), F32)
        hcar[...] = jnp.zeros_like(hcar)

    @pl.when(t > 0)
    def _():
        ubuf[0:SUBLANES, :] = ubuf[tc:tc + SUBLANES, :]

    ubuf[SUBLANES:tc + SUBLANES, :] = u_ref[...]

    cw = cw_ref[...]
    cb = cb_ref[...]
    br = br_ref[...]
    bi = bi_ref[...]
    sp = _softplus(-lam_ref[...])

    def gate_body(r, carry):
        r0 = pl.multiple_of(r * rb, rb)
        ext = ubuf[pl.ds(r0, rb + SUBLANES), :]
        conv = ext[5:rb + 5] * cw[0:1]
        conv = conv + ext[6:rb + 6] * cw[1:2]
        conv = conv + ext[7:rb + 7] * cw[2:3]
        conv = conv + ext[8:rb + 8] * cw[3:4]
        uc = cb + conv
        a, inp = _rg_gates(uc, wr_ref, wi_ref, br, bi, sp)
        for c in range(nc):
            abuf[c, pl.ds(r0, rb), :] = a[:, c * LANES:(c + 1) * LANES]
            bbuf[c, pl.ds(r0, rb), :] = inp[:, c * LANES:(c + 1) * LANES]
        return carry

    lax.fori_loop(0, tc // rb, gate_body, 0)

    def scan_body(s, carry):
        hs, accs = carry
        new_h, new_a = [], []
        for c in range(nc):
            a8 = abuf[c, pl.ds(s, SUBLANES, stride=seg), :]
            b8 = bbuf[c, pl.ds(s, SUBLANES, stride=seg), :]
            h = a8 * hs[c] + b8
            acum = accs[c] * a8
            bbuf[c, pl.ds(s, SUBLANES, stride=seg), :] = h
            abuf[c, pl.ds(s, SUBLANES, stride=seg), :] = acum
            new_h.append(h)
            new_a.append(acum)
        return tuple(new_h), tuple(new_a)

    init = (tuple(jnp.zeros((SUBLANES, LANES), F32) for _ in range(nc)),
            tuple(jnp.ones((SUBLANES, LANES), F32) for _ in range(nc)))
    h_end, a_end = lax.fori_loop(0, seg, scan_body, init)
    h_end = jnp.concatenate(h_end, axis=1)
    a_end = jnp.concatenate(a_end, axis=1)

    rows = [hcar[...]]
    for s in range(1, SUBLANES):
        rows.append(a_end[s - 1:s] * rows[-1] + h_end[s - 1:s])
    last = a_end[SUBLANES - 1:SUBLANES] * rows[-1] + h_end[SUBLANES - 1:SUBLANES]
    cin_s[...] = jnp.concatenate(rows, axis=0)
    hcar[...] = last
    hlast_ref[...] = last

    def fix_body(r, carry):
        r0 = pl.multiple_of(r * rb, rb)
        cin = cin_s[pl.ds((r * rb) // seg, 1), :]
        hloc = jnp.concatenate([bbuf[c, pl.ds(r0, rb), :] for c in range(nc)], axis=1)
        acum = jnp.concatenate([abuf[c, pl.ds(r0, rb), :] for c in range(nc)], axis=1)
        h = hloc + acum * cin
        hg_ref[pl.ds(r0, rb), :] = (h * jax.nn.gelu(yg_ref[pl.ds(r0, rb), :])).astype(BF16)
        return carry

    lax.fori_loop(0, tc // rb, fix_body, 0)


def _rglru_seq(z3, conv_w, conv_b, w_r, w_i, b_r, b_i, lam, *, tc):
    bsz, s, _ = z3.shape
    d = conv_w.shape[1]
    tc = min(tc, s)
    rb = min(128, tc // SUBLANES)
    kern = functools.partial(_rglru_seq_kernel, tc=tc, rb=rb)
    vec = pl.BlockSpec((1, d), lambda b, t: (0, 0))
    blk = pl.BlockSpec((N_RNN_BLOCKS, RNN_BLOCK, RNN_BLOCK), lambda b, t: (0, 0, 0))
    return pl.pallas_call(
        kern,
        grid=(bsz, s // tc),
        in_specs=[
            pl.BlockSpec((None, tc, d), lambda b, t: (b, t, COL_U)),
            pl.BlockSpec((None, tc, d), lambda b, t: (b, t, COL_YG)),
            pl.BlockSpec((CONV_W, d), lambda b, t: (0, 0)),
            vec, blk, blk, vec, vec, vec,
        ],
        out_specs=[
            pl.BlockSpec((None, tc, d), lambda b, t: (b, t, 0)),
            pl.BlockSpec((None, 1, d), lambda b, t: (b, 0, 0)),
        ],
        out_shape=[
            jax.ShapeDtypeStruct((bsz, s, d), BF16),
            jax.ShapeDtypeStruct((bsz, 1, d), F32),
        ],
        scratch_shapes=[
            pltpu.VMEM((tc + SUBLANES, d), F32),
            pltpu.VMEM((d // LANES, tc, LANES), F32),
            pltpu.VMEM((d // LANES, tc, LANES), F32),
            pltpu.VMEM((1, d), F32),
            pltpu.VMEM((SUBLANES, d), F32),
        ],
        compiler_params=_params(("arbitrary", "arbitrary")),
    )(z3, z3, conv_w, conv_b, w_r, w_i, b_r, b_i, lam)


def _rglru_step_kernel(u_ref, yg_ref, pre_ref, h0_ref, cw_ref, cb_ref, wr_ref, wi_ref, br_ref, bi_ref,
                       lam_ref, hg_ref, h_ref):
    cw = cw_ref[...]
    conv = pre_ref[0] * cw[0:1]
    conv = conv + pre_ref[1] * cw[1:2]
    conv = conv + pre_ref[2] * cw[2:3]
    conv = conv + u_ref[...] * cw[3:4]
    uc = cb_ref[...] + conv
    a, inp = _rg_gates(uc, wr_ref, wi_ref, br_ref[...], bi_ref[...], _softplus(-lam_ref[...]))
    h = inp + a * h0_ref[...]
    h_ref[...] = h
    hg_ref[...] = h * jax.nn.gelu(yg_ref[...])


def _rglru_step(z, prefix_t, h0, conv_w, conv_b, w_r, w_i, b_r, b_i, lam):
    n = z.shape[0]
    d = conv_w.shape[1]
    vec = pl.BlockSpec((1, d), lambda i: (0, 0))
    blk = pl.BlockSpec((N_RNN_BLOCKS, RNN_BLOCK, RNN_BLOCK), lambda i: (0, 0, 0))
    return pl.pallas_call(
        _rglru_step_kernel,
        grid=(1,),
        in_specs=[
            pl.BlockSpec((n, d), lambda i: (0, COL_U)),
            pl.BlockSpec((n, d), lambda i: (0, COL_YG)),
            pl.BlockSpec((CONV_W - 1, n, d), lambda i: (0, 0, 0)),
            pl.BlockSpec((n, d), lambda i: (0, 0)),
            pl.BlockSpec((CONV_W, d), lambda i: (0, 0)),
            vec, blk, blk, vec, vec, vec,
        ],
        out_specs=[pl.BlockSpec((n, d), lambda i: (0, 0)), pl.BlockSpec((n, d), lambda i: (0, 0))],
        out_shape=[jax.ShapeDtypeStruct((n, d), F32), jax.ShapeDtypeStruct((n, d), F32)],
        compiler_params=_params(("arbitrary",)),
    )(z, z, prefix_t, h0, conv_w, conv_b, w_r, w_i, b_r, b_i, lam)


def _attn_seq_kernel(q_ref, k_ref, v_ref, ccol_ref, crow_ref, o_ref, kb_s, vb_s, *, tq):
    h = pl.program_id(1)
    qi = pl.program_id(2)

    @pl.when(qi == 0)
    def _():
        kb_s[...] = k_ref[...].astype(BF16)
        vb_s[...] = v_ref[...].astype(BF16)

    qb = q_ref[...].astype(BF16)
    lane = lax.broadcasted_iota(jnp.int32, (tq, LANES), 1)
    cq = jnp.sum(jnp.where(lane == h, ccol_ref[...], 0.0), axis=1, keepdims=True)

    def step(j, carry, masked):
        m, l, acc = carry
        k0 = pl.multiple_of(j * tq, tq)
        s = lax.dot_general(qb, kb_s[pl.ds(k0, tq), :], NT_DIMS, preferred_element_type=F32) * ATTN_SCALE
        s = (s + cq) - crow_ref[:, pl.ds(k0, tq)]
        if masked:
            row = lax.broadcasted_iota(jnp.int32, (tq, tq), 0)
            col = lax.broadcasted_iota(jnp.int32, (tq, tq), 1)
            s = jnp.where(col <= row, s, -jnp.inf)
        m_new = jnp.maximum(m, jnp.max(s, axis=1, keepdims=True))
        alpha = jnp.exp(m - m_new)
        p = jnp.exp(s - m_new)
        l = alpha * l + jnp.sum(p, axis=1, keepdims=True)
        acc = alpha * acc + jnp.dot(p.astype(BF16), vb_s[pl.ds(k0, tq), :], preferred_element_type=F32)
        return m_new, l, acc

    init = (jnp.full((tq, 1), -jnp.inf, F32), jnp.zeros((tq, 1), F32), jnp.zeros((tq, HEAD_DIM), F32))
    carry = lax.fori_loop(0, qi, functools.partial(step, masked=False), init)
    _, l, acc = step(qi, carry, True)
    o_ref[...] = (acc / l).astype(BF16)


def _attn_seq(z3, c3, crow, *, tq):
    bsz, s, _ = z3.shape
    tq = min(tq, s)
    hb = N_HEADS
    kern = functools.partial(_attn_seq_kernel, tq=tq)
    return pl.pallas_call(
        kern,
        grid=(bsz, N_HEADS, s // tq),
        in_specs=[
            pl.BlockSpec((None, tq, HEAD_DIM), lambda b, h, i: (b, i, COL_Q * hb + h)),
            pl.BlockSpec((None, s, HEAD_DIM), lambda b, h, i: (b, 0, COL_K * hb + h)),
            pl.BlockSpec((None, s, HEAD_DIM), lambda b, h, i: (b, 0, COL_V * hb + h)),
            pl.BlockSpec((None, tq, LANES), lambda b, h, i: (b, i, 0)),
            pl.BlockSpec((None, 1, s), lambda b, h, i: (b * N_HEADS + h, 0, 0)),
        ],
        out_specs=pl.BlockSpec((None, tq, HEAD_DIM), lambda b, h, i: (b, i, h)),
        out_shape=jax.ShapeDtypeStruct((bsz, s, N_HEADS * HEAD_DIM), BF16),
        scratch_shapes=[pltpu.VMEM((s, HEAD_DIM), BF16), pltpu.VMEM((s, HEAD_DIM), BF16)],
        compiler_params=_params(("arbitrary", "arbitrary", "arbitrary")),
    )(z3, z3, z3, c3, crow)


def _attn_page_kernel(pt_ref, q_ref, kn_ref, vn_ref, lfn_ref, ux_ref, kp_ref, vp_ref, lfp_ref,
                      o_ref, m_s, l_s, acc_s, tot_s, *, n_pages):
    del pt_ref
    p = pl.program_id(1)
    page = kp_ref.shape[0]
    flat = page * N_HEADS
    qb = q_ref[...].astype(BF16)

    @pl.when(p == 0)
    def _():
        qf = qb.astype(F32)
        knf = kn_ref[...].astype(BF16).astype(F32)
        s_new = jnp.sum(qf * knf, axis=1, keepdims=True) * ATTN_SCALE
        m_s[...] = jnp.broadcast_to(s_new, m_s.shape)
        l_s[...] = jnp.ones_like(l_s)
        acc_s[...] = vn_ref[...].astype(BF16).astype(F32)
        tot_s[...] = lfn_ref[...]

    kflat = kp_ref[...].reshape(flat, HEAD_DIM).astype(BF16)
    s = lax.dot_general(qb, kflat, NT_DIMS, preferred_element_type=F32) * ATTN_SCALE
    lf = lfp_ref[...]
    hi, mid, lo = _split3(lf)
    sb = jnp.dot(jnp.concatenate([hi, mid, lo], axis=0), ux_ref[...], preferred_element_type=F32)
    suffix = (sb[0:N_HEADS] + sb[N_HEADS:2 * N_HEADS]) + sb[2 * N_HEADS:3 * N_HEADS]
    tot = tot_s[...]
    s = s + (suffix + tot[:, 0:1])
    row = lax.broadcasted_iota(jnp.int32, (N_HEADS, flat), 0)
    lane = lax.broadcasted_iota(jnp.int32, (N_HEADS, flat), 1)
    s = jnp.where(lane % N_HEADS == row, s, -jnp.inf)
    m_old = m_s[...]
    m_new = jnp.maximum(m_old, jnp.max(s, axis=1, keepdims=True))
    alpha = jnp.exp(m_old - m_new)
    pp = jnp.exp(s - m_new[:, 0:1])
    l_s[...] = alpha * l_s[...] + jnp.sum(pp, axis=1, keepdims=True)
    vflat = vp_ref[...].reshape(flat, HEAD_DIM).astype(BF16)
    acc_s[...] = alpha * acc_s[...] + jnp.dot(pp.astype(BF16), vflat, preferred_element_type=F32)
    m_s[...] = m_new
    tot_s[...] = tot + jnp.sum(lf, axis=1, keepdims=True)

    @pl.when(p == n_pages - 1)
    def _():
        o_ref[...] = acc_s[...] / l_s[...]


def _attn_page(page_table, q3, kn3, vn3, lfn3, ux, cache_k, cache_v, cache_lft, *, layer):
    nb, n_pages = page_table.shape
    page = cache_k.shape[2]
    kern = functools.partial(_attn_page_kernel, n_pages=n_pages)
    tok = pl.BlockSpec((None, N_HEADS, HEAD_DIM), lambda b, p, pt: (b, 0, 0))

    def page_idx(b, p, pt):
        return pt[b, n_pages - 1 - p]

    grid_spec = pltpu.PrefetchScalarGridSpec(
        num_scalar_prefetch=1,
        grid=(nb, n_pages),
        in_specs=[
            tok, tok, tok, tok,
            pl.BlockSpec((page, page * N_HEADS), lambda b, p, pt: (0, 0)),
            pl.BlockSpec((None, None, page, N_HEADS, HEAD_DIM),
                         lambda b, p, pt: (layer, page_idx(b, p, pt), 0, 0, 0)),
            pl.BlockSpec((None, None, page, N_HEADS, HEAD_DIM),
                         lambda b, p, pt: (layer, page_idx(b, p, pt), 0, 0, 0)),
            pl.BlockSpec((None, None, N_HEADS, page), lambda b, p, pt: (layer, page_idx(b, p, pt), 0, 0)),
        ],
        out_specs=tok,
        scratch_shapes=[pltpu.VMEM((N_HEADS, LANES), F32)] * 4,
    )
    return pl.pallas_call(
        kern,
        grid_spec=grid_spec,
        out_shape=jax.ShapeDtypeStruct((nb, N_HEADS, HEAD_DIM), F32),
        compiler_params=_params(("arbitrary", "arbitrary")),
    )(page_table, q3, kn3, vn3, lfn3, ux, cache_k, cache_v, cache_lft)


def _mix_out_kernel(hg_ref, o_ref, ga_ref, gb_ref, x_ref, wa_ref, wb_ref, wo_ref, g_ref, b_ref, y_ref):
    precise = wa_ref.dtype == F32
    ba = _mm(hg_ref[...], wa_ref[...], precise)
    bb = _mm(o_ref[...], wb_ref[...], precise)
    merged = jax.nn.sigmoid(ga_ref[...]) * ba + jax.nn.sigmoid(gb_ref[...]) * bb
    y = _mm(merged, wo_ref[...], precise)
    y_ref[...] = _layer_norm(ALPHA * x_ref[...] + y, g_ref[...], b_ref[...])


def _mix_out(hg, o, z, x, w_a, w_b, w_o, g, b, *, layer, tm):
    n, d = x.shape
    tm = min(tm, n)
    row = lambda c: pl.BlockSpec((tm, d), lambda i: (i, c))
    wsp = pl.BlockSpec((None, d, d), lambda i: (layer, 0, 0))
    vec = pl.BlockSpec((1, d), lambda i: (0, 0))
    return pl.pallas_call(
        _mix_out_kernel,
        grid=(n // tm,),
        in_specs=[row(0), row(0), row(COL_GA), row(COL_GB), row(0), wsp, wsp, wsp, vec, vec],
        out_specs=row(0),
        out_shape=jax.ShapeDtypeStruct((n, d), F32),
        compiler_params=_params(("arbitrary",)),
    )(hg, o, z, z, x, w_a, w_b, w_o, g, b)


def _pair_max(vals):
    best = None
    for i in range(len(vals)):
        for j in range(i + 1, len(vals)):
            sm = vals[i] + vals[j]
            best = sm if best is None else jnp.maximum(best, sm)
    return best


def _moe_gates(logit_rows):
    mx = functools.reduce(jnp.maximum, logit_rows)
    ex = [jnp.exp(r - mx) for r in logit_rows]
    den = functools.reduce(lambda a, b: a + b, ex)
    probs = [e / den for e in ex]
    best = None
    gsel = None
    for g in range(N_GROUPS):
        score = _pair_max(probs[g * EXPERTS_PER_GROUP:(g + 1) * EXPERTS_PER_GROUP])
        if best is None:
            best, gsel = score, jnp.zeros_like(score, dtype=jnp.int32)
        else:
            better = score > best
            gsel = jnp.where(better, g, gsel)
            best = jnp.where(better, score, best)
    masked = [jnp.where(gsel == (e // EXPERTS_PER_GROUP), logit_rows[e], -jnp.inf) for e in range(N_EXPERTS)]
    top1 = functools.reduce(jnp.maximum, masked)
    idx1 = jnp.full_like(gsel, N_EXPERTS)
    for e in reversed(range(N_EXPERTS)):
        idx1 = jnp.where(masked[e] == top1, e, idx1)
    rest = [jnp.where(idx1 == e, -jnp.inf, masked[e]) for e in range(N_EXPERTS)]
    top2 = functools.reduce(jnp.maximum, rest)
    idx2 = jnp.full_like(gsel, N_EXPERTS)
    for e in reversed(range(N_EXPERTS)):
        idx2 = jnp.where(rest[e] == top2, e, idx2)
    e2 = jnp.exp(top2 - top1)
    w1 = 1.0 / (1.0 + e2)
    w2 = e2 / (1.0 + e2)
    return [jnp.where(idx1 == e, w1, 0.0) + jnp.where(idx2 == e, w2, 0.0) for e in range(N_EXPERTS)]


def _moe_kernel(x_ref, wrt_ref, brt_ref, w1_ref, w3_ref, w2_ref, g_ref, b_ref, y_ref, xb_s, gate_s, acc_s):
    e = pl.program_id(1)
    tm = x_ref.shape[0]

    precise = w1_ref.dtype == F32

    @pl.when(e == 0)
    def _():
        x = x_ref[...]
        xb_s[...] = x.astype(xb_s.dtype)
        logits_t = lax.dot_general(wrt_ref[...], x, NT_DIMS, precision=lax.Precision.HIGHEST,
                                   preferred_element_type=F32) + brt_ref[...]
        gates = _moe_gates([logits_t[i:i + 1, :] for i in range(N_EXPERTS)])
        gate_t = jnp.concatenate(gates + [jnp.zeros((LANES - N_EXPERTS, tm), F32)], axis=0)
        gate_s[...] = gate_t.T
        acc_s[...] = jnp.zeros_like(acc_s)

    xb = xb_s[...]
    h1 = _mm(xb, w1_ref[...], precise)
    h3 = _mm(xb, w3_ref[...], precise)
    hdn = (h1 * jax.nn.sigmoid(h1)) * h3
    y = _mm(hdn, w2_ref[...], precise)
    lane = lax.broadcasted_iota(jnp.int32, (tm, LANES), 1)
    ge = jnp.sum(jnp.where(lane == e, gate_s[...], 0.0), axis=1, keepdims=True)
    acc_s[...] += ge * y

    @pl.when(e == N_EXPERTS - 1)
    def _():
        y_ref[...] = _layer_norm(ALPHA * x_ref[...] + acc_s[...], g_ref[...], b_ref[...])


def _moe(x, w_router_t, b_router_t, w_e1, w_e3, w_e2, g, b, *, layer, tm):
    n, d = x.shape
    de = w_e1.shape[-1]
    tm = min(tm, n)
    vec = pl.BlockSpec((1, d), lambda i, e: (0, 0))
    return pl.pallas_call(
        _moe_kernel,
        grid=(n // tm, N_EXPERTS),
        in_specs=[
            pl.BlockSpec((tm, d), lambda i, e: (i, 0)),
            pl.BlockSpec((N_EXPERTS, d), lambda i, e: (0, 0)),
            pl.BlockSpec((N_EXPERTS, 1), lambda i, e: (0, 0)),
            pl.BlockSpec((None, None, d, de), lambda i, e: (layer, e, 0, 0)),
            pl.BlockSpec((None, None, d, de), lambda i, e: (layer, e, 0, 0)),
            pl.BlockSpec((None, None, de, d), lambda i, e: (layer, e, 0, 0)),
            vec, vec,
        ],
        out_specs=pl.BlockSpec((tm, d), lambda i, e: (i, 0)),
        out_shape=jax.ShapeDtypeStruct((n, d), F32),
        scratch_shapes=[pltpu.VMEM((tm, d), w_e1.dtype), pltpu.VMEM((tm, LANES), F32), pltpu.VMEM((tm, d), F32)],
        compiler_params=_params(("arbitrary", "arbitrary")),
    )(x, w_router_t, b_router_t, w_e1, w_e3, w_e2, g, b)


def kernel(x_prompt, x_sample, cache_k, cache_v, cache_logf, state_h, state_conv, page_table, w_in, conv_w,
           conv_b, w_rg_r, b_rg_r, w_rg_i, b_rg_i, rg_lambda, b_forget, w_branch_a, w_branch_b, w_out,
           ln_mix_g, ln_mix_b, w_router, b_router, w_e1, w_e3, w_e2, ln_ffn_g, ln_ffn_b):
    bsz, seq, d = x_prompt.shape
    nb_s = x_sample.shape[0]
    depth = w_in.shape[0]
    page = cache_k.shape[2]
    d_rnn = conv_w.shape[-1]
    d_attn = N_HEADS * HEAD_DIM
    f_lo = 2 * d_rnn + 3 * d_attn

    xp = x_prompt.reshape(bsz * seq, d)
    xs = x_sample.reshape(nb_s, d)

    w_main = jnp.concatenate([w_in[:, :, :f_lo], w_in[:, :, f_lo + N_HEADS:]], axis=-1)
    w_f = jnp.pad(w_in[:, :, f_lo:f_lo + N_HEADS], ((0, 0), (0, 0), (0, LANES - N_HEADS)))
    b_f = jnp.pad(b_forget, ((0, 0), (0, LANES - N_HEADS)))[:, None, :]
    w_main_h, w_f_h = w_main.astype(BF16), w_f.astype(BF16)
    w_r_h, w_i_h = w_rg_r.astype(BF16), w_rg_i.astype(BF16)
    w_a_h, w_b_h, w_o_h = w_branch_a.astype(BF16), w_branch_b.astype(BF16), w_out.astype(BF16)
    w1_h, w3_h, w2_h = w_e1.astype(BF16), w_e3.astype(BF16), w_e2.astype(BF16)
    w_router_t = w_router.T
    b_router_t = b_router[:, None]
    cache_lft = jnp.swapaxes(cache_logf, -1, -2)
    tok = lax.broadcasted_iota(jnp.int32, (page, page * N_HEADS), 0)
    key = lax.broadcasted_iota(jnp.int32, (page, page * N_HEADS), 1) // N_HEADS
    ux = jnp.where(tok > key, 1.0, 0.0).astype(BF16)

    outs = [[] for _ in range(10)]
    for l in range(depth):
        vec = lambda a: a[l][None, :]
        rg_vecs = (vec(b_rg_r), vec(b_rg_i), vec(rg_lambda))

        z, lf, c = _inproj(xp, w_main_h, w_f_h, b_f, layer=l, seq_len=seq, tm=1024, tn=1024)
        z3 = z.reshape(bsz, seq, -1)
        hg, h_last = _rglru_seq(z3, conv_w[l], vec(conv_b), w_r_h[l], w_i_h[l], *rg_vecs, tc=1024)
        c3 = c.reshape(bsz, seq, LANES)
        crow = jnp.swapaxes(c3[:, :, :N_HEADS], 1, 2).reshape(bsz * N_HEADS, 1, seq)
        o = _attn_seq(z3, c3, crow, tq=512)
        x1 = _mix_out(hg.reshape(bsz * seq, d_rnn), o.reshape(bsz * seq, d_attn), z, xp,
                      w_a_h, w_b_h, w_o_h, vec(ln_mix_g), vec(ln_mix_b), layer=l, tm=512)
        xp = _moe(x1, w_router_t, b_router_t, w1_h, w3_h, w2_h, vec(ln_ffn_g), vec(ln_ffn_b), layer=l, tm=1024)
        outs[0].append(z3[:, :, COL_K * d_attn:(COL_K + 1) * d_attn].reshape(bsz, seq, N_HEADS, HEAD_DIM))
        outs[1].append(z3[:, :, COL_V * d_attn:(COL_V + 1) * d_attn].reshape(bsz, seq, N_HEADS, HEAD_DIM))
        outs[2].append(lf[:, :N_HEADS].reshape(bsz, seq, N_HEADS))
        outs[3].append(h_last.reshape(bsz, d_rnn))
        outs[4].append(z3[:, seq - (CONV_W - 1):, :d_rnn])

        zs, lfs, _ = _inproj(xs, w_main, w_f, b_f, layer=l, seq_len=None, tm=nb_s, tn=1024)
        prefix_t = jnp.swapaxes(state_conv[l], 0, 1)
        hgs, hs = _rglru_step(zs, prefix_t, state_h[l], conv_w[l], vec(conv_b), w_rg_r[l], w_rg_i[l], *rg_vecs)
        q3 = zs[:, COL_Q * d_attn:(COL_Q + 1) * d_attn].reshape(nb_s, N_HEADS, HEAD_DIM)
        kn3 = zs[:, COL_K * d_attn:(COL_K + 1) * d_attn].reshape(nb_s, N_HEADS, HEAD_DIM)
        vn3 = zs[:, COL_V * d_attn:(COL_V + 1) * d_attn].reshape(nb_s, N_HEADS, HEAD_DIM)
        lfn3 = jnp.broadcast_to(lfs[:, :N_HEADS, None], (nb_s, N_HEADS, LANES))
        os_ = _attn_page(page_table, q3, kn3, vn3, lfn3, ux, cache_k, cache_v, cache_lft, layer=l)
        x1s = _mix_out(hgs, os_.reshape(nb_s, d_attn), zs, xs,
                       w_branch_a, w_branch_b, w_out, vec(ln_mix_g), vec(ln_mix_b), layer=l, tm=nb_s)
        xs = _moe(x1s, w_router_t, b_router_t, w_e1, w_e3, w_e2, vec(ln_ffn_g), vec(ln_ffn_b), layer=l, tm=nb_s)
        outs[5].append(kn3.reshape(nb_s, 1, N_HEADS, HEAD_DIM))
        outs[6].append(vn3.reshape(nb_s, 1, N_HEADS, HEAD_DIM))
        outs[7].append(lfs[:, :N_HEADS].reshape(nb_s, 1, N_HEADS))
        outs[8].append(hs)
        outs[9].append(jnp.concatenate([state_conv[l][:, 1:], zs[:, None, :d_rnn]], axis=1))

    stacked = [jnp.stack(o_) for o_ in outs]
    return (xp.reshape(bsz, seq, d), xs.reshape(nb_s, 1, d), stacked[0], stacked[1], stacked[2], stacked[3],
            stacked[4], stacked[5], stacked[6], stacked[7], stacked[8], stacked[9])
```

```python
import functools

import jax
import jax.numpy as jnp
from jax import lax
from jax.experimental import pallas as pl
from jax.experimental.pallas import tpu as pltpu

F32 = jnp.float32
BF16 = jnp.bfloat16

LANES = 128
SUBLANES = 8
VMEM_LIMIT_BYTES = 56 * 1024 * 1024

N_HEADS = 8
HEAD_DIM = 128
N_RNN_BLOCKS = 8
RNN_BLOCK = 128
CONV_W = 4
RG_C = 8.0
N_EXPERTS = 16
N_GROUPS = 4
EXPERTS_PER_GROUP = N_EXPERTS // N_GROUPS
DEPTH = 4
ALPHA = (2 * DEPTH) ** 0.25
LN_EPS = 1e-5
ATTN_SCALE = HEAD_DIM ** -0.5

TM_INPROJ = 1024
TC_RGLRU = 1024
TQ_ATTN = 512
TM_MIX = 512
TM_MOE = 1024

LOG2E = 1.4426950408889634
Q_PRESCALE = ATTN_SCALE * LOG2E

COL_U, COL_YG, COL_GA, COL_GB = range(4)
N_F32_STREAMS = 4
J_Q, J_K, J_V = 4, 5, 6

NT_DIMS = (((1,), (1,)), ((), ()))


def _params(semantics):
    return pltpu.CompilerParams(dimension_semantics=semantics, vmem_limit_bytes=VMEM_LIMIT_BYTES)


def _mm(a, b, precise):
    if precise:
        return jnp.dot(a.astype(F32), b, precision=lax.Precision.HIGHEST, preferred_element_type=F32)
    return jnp.dot(a.astype(BF16), b, preferred_element_type=F32)


def _split3(x):
    hi = x.astype(BF16)
    r1 = x - hi.astype(F32)
    mid = r1.astype(BF16)
    lo = (r1 - mid.astype(F32)).astype(BF16)
    return hi, mid, lo


def _log_sigmoid(x):
    return jnp.minimum(x, 0.0) - jnp.log1p(jnp.exp(-jnp.abs(x)))


def _softplus(x):
    return jnp.maximum(x, 0.0) + jnp.log1p(jnp.exp(-jnp.abs(x)))


def _layer_norm(x, g, b):
    mu = jnp.mean(x, axis=-1, keepdims=True)
    xc = x - mu
    var = jnp.mean(xc * xc, axis=-1, keepdims=True)
    return xc * lax.rsqrt(var + LN_EPS) * g + b


def _inproj_kernel(*refs, tiles_per_seq, precise, n_alias):
    x_ref, w_ref, wf_ref, bf_ref = refs[:4]
    z_ref, qkv_ref, lf_ref, c_ref, k_ref, v_ref, xb_s, carry_s = refs[4 + n_alias:]
    i = pl.program_id(0)
    j = pl.program_id(1)

    @pl.when(j == 0)
    def _():
        xb = x_ref[...].astype(xb_s.dtype)
        xb_s[...] = xb
        f = _mm(xb, wf_ref[...], precise) + bf_ref[...]
        lf = _log_sigmoid(f)
        lf_ref[...] = lf

        @pl.when(i % tiles_per_seq == 0)
        def _():
            carry_s[...] = jnp.zeros_like(carry_s)

        tm = lf.shape[0]
        row = lax.broadcasted_iota(jnp.int32, (LANES, LANES), 0)
        col = lax.broadcasted_iota(jnp.int32, (LANES, LANES), 1)
        tri = jnp.where(col <= row, 1.0, 0.0).astype(BF16)
        parts = jnp.concatenate(_split3(lf), axis=1)
        carry = carry_s[...]
        for sb in range(tm // LANES):
            loc = jnp.dot(tri, parts[sb * LANES:(sb + 1) * LANES], preferred_element_type=F32)
            cb = ((loc[:, :LANES] + loc[:, LANES:2 * LANES]) + loc[:, 2 * LANES:]) + carry
            c_ref[sb * LANES:(sb + 1) * LANES, :] = cb
            carry = cb[LANES - 1:LANES, :]
        carry_s[...] = carry

    res = _mm(xb_s[...], w_ref[...], precise)

    @pl.when(j < N_F32_STREAMS)
    def _():
        z_ref[...] = res

    @pl.when(j == J_Q)
    def _():
        qkv_ref[...] = (res * Q_PRESCALE).astype(BF16)

    @pl.when(j == J_K)
    def _():
        k_ref[...] = res
        qkv_ref[...] = res.astype(BF16)

    @pl.when(j == J_V)
    def _():
        v_ref[...] = res
        qkv_ref[...] = res.astype(BF16)


def _inproj(x, w_main, w_f, b_f, k_stack, v_stack, *, layer, seq_len, tm):
    n, d = x.shape
    depth = w_main.shape[0]
    tn = d
    precise = w_main.dtype == F32
    if seq_len is None:
        tm = min(tm, n)
        tiles_per_seq = 1
    else:
        tm = min(tm, seq_len)
        tiles_per_seq = seq_len // tm
    n_alias = 0 if k_stack is None else 2
    kern = functools.partial(_inproj_kernel, tiles_per_seq=tiles_per_seq, precise=precise, n_alias=n_alias)
    stack_spec = pl.BlockSpec((None, tm, tn), lambda i, j: (layer, i, 0))
    in_specs = [
        pl.BlockSpec((tm, d), lambda i, j: (i, 0)),
        pl.BlockSpec((None, d, tn), lambda i, j: (layer, 0, j)),
        pl.BlockSpec((None, d, LANES), lambda i, j: (layer, 0, 0)),
        pl.BlockSpec((None, 1, LANES), lambda i, j: (layer, 0, 0)),
    ] + [pl.BlockSpec(memory_space=pl.ANY)] * n_alias
    args = (x, w_main, w_f, b_f) + (() if k_stack is None else (k_stack, v_stack))
    return pl.pallas_call(
        kern,
        grid=(n // tm, N_F32_STREAMS + 3),
        in_specs=in_specs,
        out_specs=[
            pl.BlockSpec((tm, tn), lambda i, j: (i, jnp.minimum(j, N_F32_STREAMS - 1))),
            pl.BlockSpec((tm, tn), lambda i, j: (i, jnp.clip(j - J_Q, 0, 2))),
            pl.BlockSpec((tm, LANES), lambda i, j: (i, 0)),
            pl.BlockSpec((tm, LANES), lambda i, j: (i, 0)),
            stack_spec, stack_spec,
        ],
        out_shape=[
            jax.ShapeDtypeStruct((n, N_F32_STREAMS * tn), F32),
            jax.ShapeDtypeStruct((n, 3 * tn), BF16),
            jax.ShapeDtypeStruct((n, LANES), F32),
            jax.ShapeDtypeStruct((n, LANES), F32),
            jax.ShapeDtypeStruct((depth, n, tn), F32),
            jax.ShapeDtypeStruct((depth, n, tn), F32),
        ],
        input_output_aliases={} if k_stack is None else {4: 4, 5: 5},
        scratch_shapes=[pltpu.VMEM((tm, d), w_main.dtype), pltpu.VMEM((1, LANES), F32)],
        compiler_params=_params(("arbitrary", "arbitrary")),
    )(*args)


def _rg_gates(uc, wr_ref, wi_ref, br, bi, sp_neg_lam):
    precise = wr_ref.dtype == F32
    ucb = uc.astype(wr_ref.dtype)
    rs, gs = [], []
    for n in range(N_RNN_BLOCKS):
        blk = ucb[:, n * RNN_BLOCK:(n + 1) * RNN_BLOCK]
        rs.append(_mm(blk, wr_ref[n], precise))
        gs.append(_mm(blk, wi_ref[n], precise))
    r = jax.nn.sigmoid(jnp.concatenate(rs, axis=1) + br)
    ig = jax.nn.sigmoid(jnp.concatenate(gs, axis=1) + bi)
    log_a = (-RG_C) * r * sp_neg_lam
    a = jnp.exp(log_a)
    inp = (ig * uc) * jnp.sqrt(1.0 - a * a)
    return a, inp


def _rglru_seq_kernel(u_ref, yg_ref, cw_ref, cb_ref, wr_ref, wi_ref, br_ref, bi_ref, lam_ref,
                      hg_ref, hlast_ref, ubuf, abuf, bbuf, hcar, cin_s, *, tc, rb):
    t = pl.program_id(1)
    d = u_ref.shape[-1]
    seg = tc // SUBLANES
    nc = d // LANES

    @pl.when(t == 0)
    def _():
        ubuf[0:SUBLANES, :] = jnp.zeros((SUBLANES, d), F32)
        hcar[...] = jnp.zeros_like(hcar)

    @pl.when(t > 0)
    def _():
        ubuf[0:SUBLANES, :] = ubuf[tc:tc + SUBLANES, :]

    ubuf[SUBLANES:tc + SUBLANES, :] = u_ref[...]

    cw = cw_ref[...]
    cb = cb_ref[...]
    br = br_ref[...]
    bi = bi_ref[...]
    sp = _softplus(-lam_ref[...])

    def gate_body(r, carry):
        r0 = pl.multiple_of(r * rb, rb)
        ext = ubuf[pl.ds(r0, rb + SUBLANES), :]
        conv = ext[5:rb + 5] * cw[0:1]
        conv = conv + ext[6:rb + 6] * cw[1:2]
        conv = conv + ext[7:rb + 7] * cw[2:3]
        conv = conv + ext[8:rb + 8] * cw[3:4]
        uc = cb + conv
        a, inp = _rg_gates(uc, wr_ref, wi_ref, br, bi, sp)
        for c in range(nc):
            abuf[c, pl.ds(r0, rb), :] = a[:, c * LANES:(c + 1) * LANES]
            bbuf[c, pl.ds(r0, rb), :] = inp[:, c * LANES:(c + 1) * LANES]
        return carry

    lax.fori_loop(0, tc // rb, gate_body, 0)

    def scan_body(s, carry):
        hs, accs = carry
        new_h, new_a = [], []
        for c in range(nc):
            a8 = abuf[c, pl.ds(s, SUBLANES, stride=seg), :]
            b8 = bbuf[c, pl.ds(s, SUBLANES, stride=seg), :]
            h = a8 * hs[c] + b8
            acum = accs[c] * a8
            bbuf[c, pl.ds(s, SUBLANES, stride=seg), :] = h
            abuf[c, pl.ds(s, SUBLANES, stride=seg), :] = acum
            new_h.append(h)
            new_a.append(acum)
        return tuple(new_h), tuple(new_a)

    init = (tuple(jnp.zeros((SUBLANES, LANES), F32) for _ in range(nc)),
            tuple(jnp.ones((SUBLANES, LANES), F32) for _ in range(nc)))
    h_end, a_end = lax.fori_loop(0, seg, scan_body, init)
    h_end = jnp.concatenate(h_end, axis=1)
    a_end = jnp.concatenate(a_end, axis=1)

    rows = [hcar[...]]
    for s in range(1, SUBLANES):
        rows.append(a_end[s - 1:s] * rows[-1] + h_end[s - 1:s])
    last = a_end[SUBLANES - 1:SUBLANES] * rows[-1] + h_end[SUBLANES - 1:SUBLANES]
    cin_s[...] = jnp.concatenate(rows, axis=0)
    hcar[...] = last
    hlast_ref[...] = last

    def fix_body(r, carry):
        r0 = pl.multiple_of(r * rb, rb)
        cin = cin_s[pl.ds((r * rb) // seg, 1), :]
        hloc = jnp.concatenate([bbuf[c, pl.ds(r0, rb), :] for c in range(nc)], axis=1)
        acum = jnp.concatenate([abuf[c, pl.ds(r0, rb), :] for c in range(nc)], axis=1)
        h = hloc + acum * cin
        hg_ref[pl.ds(r0, rb), :] = (h * jax.nn.gelu(yg_ref[pl.ds(r0, rb), :])).astype(BF16)
        return carry

    lax.fori_loop(0, tc // rb, fix_body, 0)


def _rglru_seq(z3, conv_w, conv_b, w_r, w_i, b_r, b_i, lam, *, tc):
    bsz, s, _ = z3.shape
    d = conv_w.shape[1]
    tc = min(tc, s)
    rb = min(128, tc // SUBLANES)
    kern = functools.partial(_rglru_seq_kernel, tc=tc, rb=rb)
    vec = pl.BlockSpec((1, d), lambda b, t: (0, 0))
    blk = pl.BlockSpec((N_RNN_BLOCKS, RNN_BLOCK, RNN_BLOCK), lambda b, t: (0, 0, 0))
    return pl.pallas_call(
        kern,
        grid=(bsz, s // tc),
        in_specs=[
            pl.BlockSpec((None, tc, d), lambda b, t: (b, t, COL_U)),
            pl.BlockSpec((None, tc, d), lambda b, t: (b, t, COL_YG)),
            pl.BlockSpec((CONV_W, d), lambda b, t: (0, 0)),
            vec, blk, blk, vec, vec, vec,
        ],
        out_specs=[
            pl.BlockSpec((None, tc, d), lambda b, t: (b, t, 0)),
            pl.BlockSpec((None, 1, d), lambda b, t: (b, 0, 0)),
        ],
        out_shape=[
            jax.ShapeDtypeStruct((bsz, s, d), BF16),
            jax.ShapeDtypeStruct((bsz, 1, d), F32),
        ],
        scratch_shapes=[
            pltpu.VMEM((tc + SUBLANES, d), F32),
            pltpu.VMEM((d // LANES, tc, LANES), F32),
            pltpu.VMEM((d // LANES, tc, LANES), F32),
            pltpu.VMEM((1, d), F32),
            pltpu.VMEM((SUBLANES, d), F32),
        ],
        compiler_params=_params(("arbitrary", "arbitrary")),
    )(z3, z3, conv_w, conv_b, w_r, w_i, b_r, b_i, lam)


def _rglru_step_kernel(u_ref, yg_ref, pre_ref, h0_ref, cw_ref, cb_ref, wr_ref, wi_ref, br_ref, bi_ref,
                       lam_ref, hg_ref, h_ref):
    cw = cw_ref[...]
    conv = pre_ref[0] * cw[0:1]
    conv = conv + pre_ref[1] * cw[1:2]
    conv = conv + pre_ref[2] * cw[2:3]
    conv = conv + u_ref[...] * cw[3:4]
    uc = cb_ref[...] + conv
    a, inp = _rg_gates(uc, wr_ref, wi_ref, br_ref[...], bi_ref[...], _softplus(-lam_ref[...]))
    h = inp + a * h0_ref[...]
    h_ref[...] = h
    hg_ref[...] = h * jax.nn.gelu(yg_ref[...])


def _rglru_step(z, prefix_t, h0, conv_w, conv_b, w_r, w_i, b_r, b_i, lam):
    n = z.shape[0]
    d = conv_w.shape[1]
    vec = pl.BlockSpec((1, d), lambda i: (0, 0))
    blk = pl.BlockSpec((N_RNN_BLOCKS, RNN_BLOCK, RNN_BLOCK), lambda i: (0, 0, 0))
    return pl.pallas_call(
        _rglru_step_kernel,
        grid=(1,),
        in_specs=[
            pl.BlockSpec((n, d), lambda i: (0, COL_U)),
            pl.BlockSpec((n, d), lambda i: (0, COL_YG)),
            pl.BlockSpec((CONV_W - 1, n, d), lambda i: (0, 0, 0)),
            pl.BlockSpec((n, d), lambda i: (0, 0)),
            pl.BlockSpec((CONV_W, d), lambda i: (0, 0)),
            vec, blk, blk, vec, vec, vec,
        ],
        out_specs=[pl.BlockSpec((n, d), lambda i: (0, 0)), pl.BlockSpec((n, d), lambda i: (0, 0))],
        out_shape=[jax.ShapeDtypeStruct((n, d), F32), jax.ShapeDtypeStruct((n, d), F32)],
        compiler_params=_params(("arbitrary",)),
    )(z, z, prefix_t, h0, conv_w, conv_b, w_r, w_i, b_r, b_i, lam)


def _attn_seq_kernel(q_ref, k_ref, v_ref, crow_ref, o_ref, v1_s, *, tq):
    qi = pl.program_id(2)

    @pl.when(qi == 0)
    def _():
        v1_s[:, 0:HEAD_DIM] = v_ref[...]
        v1_s[:, HEAD_DIM:2 * HEAD_DIM] = jnp.ones((v1_s.shape[0], HEAD_DIM), BF16)

    qb = q_ref[...]

    def step(j, carry, masked):
        m, l, acc = carry
        k0 = pl.multiple_of(j * tq, tq)
        s = lax.dot_general(qb, k_ref[pl.ds(k0, tq), :], NT_DIMS, preferred_element_type=F32)
        t = s - crow_ref[:, pl.ds(k0, tq)] * LOG2E
        if masked:
            row = lax.broadcasted_iota(jnp.int32, (tq, tq), 0)
            col = lax.broadcasted_iota(jnp.int32, (tq, tq), 1)
            t = jnp.where(col <= row, t, -jnp.inf)
        m_new = jnp.maximum(m, jnp.max(t, axis=1, keepdims=True))
        alpha = jnp.exp2(m - m_new)
        p = jnp.exp2(t - m_new).astype(BF16)
        pv = jnp.dot(p, v1_s[pl.ds(k0, tq), :], preferred_element_type=F32)
        acc = alpha * acc + pv[:, 0:HEAD_DIM]
        l = alpha * l + pv[:, HEAD_DIM:2 * HEAD_DIM]
        return m_new, l, acc

    init = (jnp.full((tq, 1), -jnp.inf, F32), jnp.zeros((tq, HEAD_DIM), F32), jnp.zeros((tq, HEAD_DIM), F32))
    carry = lax.fori_loop(0, qi, functools.partial(step, masked=False), init)
    _, l, acc = step(qi, carry, True)
    o_ref[...] = (acc / l).astype(BF16)


def _attn_seq(qkv3, crow, *, tq):
    bsz, s, _ = qkv3.shape
    tq = min(tq, s)
    kern = functools.partial(_attn_seq_kernel, tq=tq)
    return pl.pallas_call(
        kern,
        grid=(bsz, N_HEADS, s // tq),
        in_specs=[
            pl.BlockSpec((None, tq, HEAD_DIM), lambda b, h, i: (b, i, h)),
            pl.BlockSpec((None, s, HEAD_DIM), lambda b, h, i: (b, 0, N_HEADS + h)),
            pl.BlockSpec((None, s, HEAD_DIM), lambda b, h, i: (b, 0, 2 * N_HEADS + h)),
            pl.BlockSpec((None, 1, s), lambda b, h, i: (b * N_HEADS + h, 0, 0)),
        ],
        out_specs=pl.BlockSpec((None, tq, HEAD_DIM), lambda b, h, i: (b, i, h)),
        out_shape=jax.ShapeDtypeStruct((bsz, s, N_HEADS * HEAD_DIM), BF16),
        scratch_shapes=[pltpu.VMEM((s, 2 * HEAD_DIM), BF16)],
        compiler_params=_params(("arbitrary", "arbitrary", "arbitrary")),
    )(qkv3, qkv3, qkv3, crow)


def _attn_page_kernel(pt_ref, q_ref, kn_ref, vn_ref, lfn_ref, ux_ref, *refs, n_pages):
    del pt_ref
    k_refs = refs[0:n_pages]
    v_refs = refs[n_pages:2 * n_pages]
    lf_refs = refs[2 * n_pages:3 * n_pages]
    o_ref = refs[3 * n_pages]
    page = k_refs[0].shape[0]
    flat = page * N_HEADS
    qb = q_ref[...].astype(BF16)
    knf = kn_ref[...].astype(BF16).astype(F32)
    s_new = jnp.sum(qb.astype(F32) * knf, axis=1, keepdims=True)

    lfs = [r[...] for r in lf_refs]
    parts = []
    for lf in lfs:
        parts.extend(_split3(lf))
    sb = jnp.dot(jnp.concatenate(parts, axis=0), ux_ref[...], preferred_element_type=F32)
    tot = lfn_ref[...][:, 0:1]
    biases = [None] * n_pages
    for p in reversed(range(n_pages)):
        r0 = 3 * N_HEADS * p
        suffix = (sb[r0:r0 + N_HEADS] + sb[r0 + N_HEADS:r0 + 2 * N_HEADS]) + sb[r0 + 2 * N_HEADS:r0 + 3 * N_HEADS]
        biases[p] = (suffix + tot) * LOG2E
        tot = tot + jnp.sum(lfs[p], axis=1, keepdims=True)

    row = lax.broadcasted_iota(jnp.int32, (N_HEADS, flat), 0)
    lane = lax.broadcasted_iota(jnp.int32, (N_HEADS, flat), 1)
    own_head = lane % N_HEADS == row
    scores = []
    for p in range(n_pages):
        kflat = k_refs[p][...].reshape(flat, HEAD_DIM).astype(BF16)
        s = lax.dot_general(qb, kflat, NT_DIMS, preferred_element_type=F32) + biases[p]
        scores.append(jnp.where(own_head, s, -jnp.inf))
    m = jnp.maximum(s_new, jnp.max(functools.reduce(jnp.maximum, scores), axis=1, keepdims=True))
    e_new = jnp.exp2(s_new - m)
    acc = e_new * vn_ref[...].astype(BF16).astype(F32)
    psum = jnp.zeros((N_HEADS, flat), F32)
    for p in range(n_pages):
        pp = jnp.exp2(scores[p] - m)
        psum = psum + pp
        vflat = v_refs[p][...].reshape(flat, HEAD_DIM).astype(BF16)
        acc = acc + jnp.dot(pp.astype(BF16), vflat, preferred_element_type=F32)
    o_ref[...] = acc / (e_new + jnp.sum(psum, axis=1, keepdims=True))


def _attn_page(page_table, q3, kn3, vn3, lfn3, ux, cache_k, cache_v, cache_lft, *, layer):
    nb, n_pages = page_table.shape
    page = cache_k.shape[2]
    kern = functools.partial(_attn_page_kernel, n_pages=n_pages)
    tok = pl.BlockSpec((None, N_HEADS, HEAD_DIM), lambda b, pt: (b, 0, 0))

    def kv_spec(p):
        return pl.BlockSpec((None, None, page, N_HEADS, HEAD_DIM), lambda b, pt: (layer, pt[b, p], 0, 0, 0))

    def lf_spec(p):
        return pl.BlockSpec((None, None, N_HEADS, page), lambda b, pt: (layer, pt[b, p], 0, 0))

    pages = range(n_pages)
    grid_spec = pltpu.PrefetchScalarGridSpec(
        num_scalar_prefetch=1,
        grid=(nb,),
        in_specs=([tok, tok, tok, tok, pl.BlockSpec((page, page * N_HEADS), lambda b, pt: (0, 0))]
                  + [kv_spec(p) for p in pages] + [kv_spec(p) for p in pages] + [lf_spec(p) for p in pages]),
        out_specs=tok,
    )
    return pl.pallas_call(
        kern,
        grid_spec=grid_spec,
        out_shape=jax.ShapeDtypeStruct((nb, N_HEADS, HEAD_DIM), F32),
        compiler_params=_params(("arbitrary",)),
    )(page_table, q3, kn3, vn3, lfn3, ux, *([cache_k] * n_pages), *([cache_v] * n_pages),
      *([cache_lft] * n_pages))


def _mix_out_kernel(hg_ref, o_ref, ga_ref, gb_ref, x_ref, wa_ref, wb_ref, wo_ref, g_ref, b_ref, y_ref):
    precise = wa_ref.dtype == F32
    ba = _mm(hg_ref[...], wa_ref[...], precise)
    bb = _mm(o_ref[...], wb_ref[...], precise)
    merged = jax.nn.sigmoid(ga_ref[...]) * ba + jax.nn.sigmoid(gb_ref[...]) * bb
    y = _mm(merged, wo_ref[...], precise)
    y_ref[...] = _layer_norm(ALPHA * x_ref[...] + y, g_ref[...], b_ref[...])


def _mix_out(hg, o, z, x, w_a, w_b, w_o, g, b, *, layer, tm):
    n, d = x.shape
    tm = min(tm, n)
    row = lambda c: pl.BlockSpec((tm, d), lambda i: (i, c))
    wsp = pl.BlockSpec((None, d, d), lambda i: (layer, 0, 0))
    vec = pl.BlockSpec((1, d), lambda i: (0, 0))
    return pl.pallas_call(
        _mix_out_kernel,
        grid=(n // tm,),
        in_specs=[row(0), row(0), row(COL_GA), row(COL_GB), row(0), wsp, wsp, wsp, vec, vec],
        out_specs=row(0),
        out_shape=jax.ShapeDtypeStruct((n, d), F32),
        compiler_params=_params(("arbitrary",)),
    )(hg, o, z, z, x, w_a, w_b, w_o, g, b)


def _pair_max(vals):
    best = None
    for i in range(len(vals)):
        for j in range(i + 1, len(vals)):
            sm = vals[i] + vals[j]
            best = sm if best is None else jnp.maximum(best, sm)
    return best


def _moe_gates(logit_rows):
    mx = functools.reduce(jnp.maximum, logit_rows)
    ex = [jnp.exp(r - mx) for r in logit_rows]
    den = functools.reduce(lambda a, b: a + b, ex)
    probs = [e / den for e in ex]
    best = None
    gsel = None
    for g in range(N_GROUPS):
        score = _pair_max(probs[g * EXPERTS_PER_GROUP:(g + 1) * EXPERTS_PER_GROUP])
        if best is None:
            best, gsel = score, jnp.zeros_like(score, dtype=jnp.int32)
        else:
            better = score > best
            gsel = jnp.where(better, g, gsel)
            best = jnp.where(better, score, best)
    masked = [jnp.where(gsel == (e // EXPERTS_PER_GROUP), logit_rows[e], -jnp.inf) for e in range(N_EXPERTS)]
    top1 = functools.reduce(jnp.maximum, masked)
    idx1 = jnp.full_like(gsel, N_EXPERTS)
    for e in reversed(range(N_EXPERTS)):
        idx1 = jnp.where(masked[e] == top1, e, idx1)
    rest = [jnp.where(idx1 == e, -jnp.inf, masked[e]) for e in range(N_EXPERTS)]
    top2 = functools.reduce(jnp.maximum, rest)
    idx2 = jnp.full_like(gsel, N_EXPERTS)
    for e in reversed(range(N_EXPERTS)):
        idx2 = jnp.where(rest[e] == top2, e, idx2)
    e2 = jnp.exp(top2 - top1)
    w1 = 1.0 / (1.0 + e2)
    w2 = e2 / (1.0 + e2)
    return [jnp.where(idx1 == e, w1, 0.0) + jnp.where(idx2 == e, w2, 0.0) for e in range(N_EXPERTS)]


def _moe_kernel(x_ref, wrt_ref, brt_ref, w1_ref, w3_ref, w2_ref, g_ref, b_ref, y_ref, xb_s, gate_s, acc_s):
    e = pl.program_id(1)
    tm = x_ref.shape[0]

    precise = w1_ref.dtype == F32

    @pl.when(e == 0)
    def _():
        x = x_ref[...]
        xb_s[...] = x.astype(xb_s.dtype)
        logits_t = lax.dot_general(wrt_ref[...], x, NT_DIMS, precision=lax.Precision.HIGHEST,
                                   preferred_element_type=F32) + brt_ref[...]
        gates = _moe_gates([logits_t[i:i + 1, :] for i in range(N_EXPERTS)])
        gate_t = jnp.concatenate(gates + [jnp.zeros((LANES - N_EXPERTS, tm), F32)], axis=0)
        gate_s[...] = gate_t.T
        acc_s[...] = jnp.zeros_like(acc_s)

    xb = xb_s[...]
    h1 = _mm(xb, w1_ref[...], precise)
    h3 = _mm(xb, w3_ref[...], precise)
    hdn = (h1 * jax.nn.sigmoid(h1)) * h3
    y = _mm(hdn, w2_ref[...], precise)
    lane = lax.broadcasted_iota(jnp.int32, (tm, LANES), 1)
    ge = jnp.sum(jnp.where(lane == e, gate_s[...], 0.0), axis=1, keepdims=True)
    acc_s[...] += ge * y

    @pl.when(e == N_EXPERTS - 1)
    def _():
        y_ref[...] = _layer_norm(ALPHA * x_ref[...] + acc_s[...], g_ref[...], b_ref[...])


def _moe(x, w_router_t, b_router_t, w_e1, w_e3, w_e2, g, b, *, layer, tm):
    n, d = x.shape
    de = w_e1.shape[-1]
    tm = min(tm, n)
    vec = pl.BlockSpec((1, d), lambda i, e: (0, 0))
    return pl.pallas_call(
        _moe_kernel,
        grid=(n // tm, N_EXPERTS),
        in_specs=[
            pl.BlockSpec((tm, d), lambda i, e: (i, 0)),
            pl.BlockSpec((N_EXPERTS, d), lambda i, e: (0, 0)),
            pl.BlockSpec((N_EXPERTS, 1), lambda i, e: (0, 0)),
            pl.BlockSpec((None, None, d, de), lambda i, e: (layer, e, 0, 0)),
            pl.BlockSpec((None, None, d, de), lambda i, e: (layer, e, 0, 0)),
            pl.BlockSpec((None, None, de, d), lambda i, e: (layer, e, 0, 0)),
            vec, vec,
        ],
        out_specs=pl.BlockSpec((tm, d), lambda i, e: (i, 0)),
        out_shape=jax.ShapeDtypeStruct((n, d), F32),
        scratch_shapes=[pltpu.VMEM((tm, d), w_e1.dtype), pltpu.VMEM((tm, LANES), F32), pltpu.VMEM((tm, d), F32)],
        compiler_params=_params(("arbitrary", "arbitrary")),
    )(x, w_router_t, b_router_t, w_e1, w_e3, w_e2, g, b)


def kernel(x_prompt, x_sample, cache_k, cache_v, cache_logf, state_h, state_conv, page_table, w_in, conv_w,
           conv_b, w_rg_r, b_rg_r, w_rg_i, b_rg_i, rg_lambda, b_forget, w_branch_a, w_branch_b, w_out,
           ln_mix_g, ln_mix_b, w_router, b_router, w_e1, w_e3, w_e2, ln_ffn_g, ln_ffn_b):
    bsz, seq, d = x_prompt.shape
    nb_s = x_sample.shape[0]
    depth = w_in.shape[0]
    page = cache_k.shape[2]
    d_rnn = conv_w.shape[-1]
    d_attn = N_HEADS * HEAD_DIM
    f_lo = 2 * d_rnn + 3 * d_attn

    xp = x_prompt.reshape(bsz * seq, d)
    xs = x_sample.reshape(nb_s, d)

    qkv_lo = 2 * d_rnn
    w_main = jnp.concatenate([w_in[:, :, :qkv_lo], w_in[:, :, f_lo + N_HEADS:], w_in[:, :, qkv_lo:f_lo]], axis=-1)
    w_f = jnp.pad(w_in[:, :, f_lo:f_lo + N_HEADS], ((0, 0), (0, 0), (0, LANES - N_HEADS)))
    b_f = jnp.pad(b_forget, ((0, 0), (0, LANES - N_HEADS)))[:, None, :]
    w_main_h, w_f_h = w_main.astype(BF16), w_f.astype(BF16)
    w_r_h, w_i_h = w_rg_r.astype(BF16), w_rg_i.astype(BF16)
    w_a_h, w_b_h, w_o_h = w_branch_a.astype(BF16), w_branch_b.astype(BF16), w_out.astype(BF16)
    w1_h, w3_h, w2_h = w_e1.astype(BF16), w_e3.astype(BF16), w_e2.astype(BF16)
    w_router_t = w_router.T
    b_router_t = b_router[:, None]
    cache_lft = jnp.swapaxes(cache_logf, -1, -2)
    tok = lax.broadcasted_iota(jnp.int32, (page, page * N_HEADS), 0)
    key = lax.broadcasted_iota(jnp.int32, (page, page * N_HEADS), 1) // N_HEADS
    ux = jnp.where(tok > key, 1.0, 0.0).astype(BF16)

    outs = [[] for _ in range(8)]
    kp = vp = ks = vs = None
    for l in range(depth):
        vec = lambda a: a[l][None, :]
        rg_vecs = (vec(b_rg_r), vec(b_rg_i), vec(rg_lambda))

        z, qkv, lf, c, kp, vp = _inproj(xp, w_main_h, w_f_h, b_f, kp, vp, layer=l, seq_len=seq, tm=TM_INPROJ)
        z3 = z.reshape(bsz, seq, -1)
        hg, h_last = _rglru_seq(z3, conv_w[l], vec(conv_b), w_r_h[l], w_i_h[l], *rg_vecs, tc=TC_RGLRU)
        crow = jnp.swapaxes(c.reshape(bsz, seq, LANES)[:, :, :N_HEADS], 1, 2).reshape(bsz * N_HEADS, 1, seq)
        o = _attn_seq(qkv.reshape(bsz, seq, -1), crow, tq=TQ_ATTN)
        x1 = _mix_out(hg.reshape(bsz * seq, d_rnn), o.reshape(bsz * seq, d_attn), z, xp,
                      w_a_h, w_b_h, w_o_h, vec(ln_mix_g), vec(ln_mix_b), layer=l, tm=TM_MIX)
        xp = _moe(x1, w_router_t, b_router_t, w1_h, w3_h, w2_h, vec(ln_ffn_g), vec(ln_ffn_b), layer=l, tm=TM_MOE)
        outs[0].append(lf[:, :N_HEADS].reshape(bsz, seq, N_HEADS))
        outs[1].append(h_last.reshape(bsz, d_rnn))
        outs[2].append(z3[:, seq - (CONV_W - 1):, :d_rnn])

        zs, qkvs, lfs, _, ks, vs = _inproj(xs, w_main, w_f, b_f, ks, vs, layer=l, seq_len=None, tm=nb_s)
        prefix_t = jnp.swapaxes(state_conv[l], 0, 1)
        hgs, hs = _rglru_step(zs, prefix_t, state_h[l], conv_w[l], vec(conv_b), w_rg_r[l], w_rg_i[l], *rg_vecs)
        q3 = qkvs[:, :d_attn].astype(F32).reshape(nb_s, N_HEADS, HEAD_DIM)
        kn3 = ks[l].reshape(nb_s, N_HEADS, HEAD_DIM)
        vn3 = vs[l].reshape(nb_s, N_HEADS, HEAD_DIM)
        lfn3 = jnp.broadcast_to(lfs[:, :N_HEADS, None], (nb_s, N_HEADS, LANES))
        os_ = _attn_page(page_table, q3, kn3, vn3, lfn3, ux, cache_k, cache_v, cache_lft, layer=l)
        x1s = _mix_out(hgs, os_.reshape(nb_s, d_attn), zs, xs,
                       w_branch_a, w_branch_b, w_out, vec(ln_mix_g), vec(ln_mix_b), layer=l, tm=nb_s)
        xs = _moe(x1s, w_router_t, b_router_t, w_e1, w_e3, w_e2, vec(ln_ffn_g), vec(ln_ffn_b), layer=l, tm=nb_s)
        outs[3].append(lfs[:, :N_HEADS].reshape(nb_s, 1, N_HEADS))
        outs[4].append(hs)
        outs[5].append(jnp.concatenate([state_conv[l][:, 1:], zs[:, None, :d_rnn]], axis=1))

    stacked = [jnp.stack(o_) for o_ in outs[:6]]
    return (xp.reshape(bsz, seq, d), xs.reshape(nb_s, 1, d),
            kp.reshape(depth, bsz, seq, N_HEADS, HEAD_DIM), vp.reshape(depth, bsz, seq, N_HEADS, HEAD_DIM),
            stacked[0], stacked[1], stacked[2],
            ks.reshape(depth, nb_s, 1, N_HEADS, HEAD_DIM), vs.reshape(depth, nb_s, 1, N_HEADS, HEAD_DIM),
            stacked[3], stacked[4], stacked[5])
```
